```python
import math
import jax, jax.numpy as jnp
from jax import lax
import numpy as np

D_MODEL = 4096
BATCH = 16
SEQ = 256
DEPTH = 2
DEC_BATCH = 4
DEC_SEQ = 4096
PAST_LEN = 256

GRID_W = 64
BLK = 128
EPS = 1e-6
NEG = -1e30
ROPE_THETA = 10000.0
MIX = D_MODEL
GROUP_W = MIX // 4
HD = 128
CONV_CH = GROUP_W
CONV_K = 31
WIN_H = GROUP_W // HD
WIN_KV = 2
WINDOW = 128
GLOB_H = GROUP_W // HD
GLOB_KV = 2
MLA_H = GROUP_W // HD
MLA_NOPE = 128
MLA_ROPE = 64
MLA_V = 128
MLA_QK = MLA_NOPE + MLA_ROPE
Q_LORA = 768
KV_LORA = 256
PEER_HEADS = 8
PEER_DQ = 256
N_KEYS = 128
N_EXPERTS = N_KEYS * N_KEYS
PEER_TOPK = 16
PEER_TBLK = 64
IN_SIZES = (2 * CONV_CH, WIN_H * HD, WIN_KV * HD, WIN_KV * HD, GLOB_H * HD, GLOB_KV * HD, GLOB_KV * HD, Q_LORA, KV_LORA, MLA_ROPE)
IN_COLS = sum(IN_SIZES)

kernel_name = "hybrid_dit_prefix_context_step"


def rmsnorm(x, g):
    xf = x.astype(jnp.float32)
    y = xf * lax.rsqrt(jnp.mean(xf * xf, -1, keepdims=True) + EPS)
    return (y * g.astype(jnp.float32)).astype(x.dtype)


def layernorm(x, g, b):
    xf = x.astype(jnp.float32)
    mu = jnp.mean(xf, -1, keepdims=True)
    var = jnp.mean(jnp.square(xf - mu), -1, keepdims=True)
    y = (xf - mu) * lax.rsqrt(var + EPS)
    return (y * g.astype(jnp.float32) + b.astype(jnp.float32)).astype(x.dtype)


def split_cols(p):
    idx, off = [], 0
    for s in IN_SIZES[:-1]:
        off += s
        idx.append(off)
    return jnp.split(p, idx, axis=-1)


def rope_tables(rows, dim):
    half = dim // 2
    nf = half // 2
    inv = ROPE_THETA ** (-(jnp.arange(nf, dtype=jnp.float32) * (2.0 / half)))
    row = jnp.repeat(jnp.arange(rows, dtype=jnp.float32), GRID_W)
    col = jnp.tile(jnp.arange(GRID_W, dtype=jnp.float32), rows)
    ang = jnp.concatenate([row[:, None] * inv, col[:, None] * inv], -1)
    return jnp.cos(ang), jnp.sin(ang)


def apply_rope(x, cos, sin):
    d2 = x.shape[-1] // 2
    x1 = x[..., :d2].astype(jnp.float32)
    x2 = x[..., d2:].astype(jnp.float32)
    c = cos[None, :, None, :]
    s = sin[None, :, None, :]
    return jnp.concatenate([x1 * c - x2 * s, x2 * c + x1 * s], -1).astype(x.dtype)


def rope_tail(x, cos, sin):
    return jnp.concatenate([x[..., :MLA_NOPE], apply_rope(x[..., MLA_NOPE:], cos, sin)], -1)


def group_heads(q, n_kv):
    B, T, H, D = q.shape
    return q.reshape(B, T, n_kv, H // n_kv, D)


def attend(q, k, v, mask=None, sink=None):
    scale = q.shape[-1] ** -0.5
    s = jnp.einsum('bqhgd,bkhd->bhgqk', q, k, preferred_element_type=jnp.float32) * scale
    if mask is not None:
        s = jnp.where(mask, s, NEG)
    m = jnp.max(s, -1, keepdims=True)
    if sink is not None:
        sk = sink.astype(jnp.float32)[None, :, :, None, None]
        m = jnp.maximum(m, sk)
        p = jnp.exp(s - m)
        den = jnp.sum(p, -1, keepdims=True) + jnp.exp(sk - m)
    else:
        p = jnp.exp(s - m)
        den = jnp.sum(p, -1, keepdims=True)
    p = (p / den).astype(v.dtype)
    return jnp.einsum('bhgqk,bkhd->bqhgd', p, v)


def dense_attention(q, k, v, sink=None):
    B, T = q.shape[:2]
    nb = T // BLK
    qb = jnp.moveaxis(q.reshape((B, nb, BLK) + q.shape[2:]), 1, 0)
    out = lax.map(lambda qi: attend(qi, k, v, None, sink), qb)
    return jnp.moveaxis(out, 0, 1).reshape((B, T) + out.shape[3:])


def banded_attention(q, k, v, kc, vc, sink):
    B, T = q.shape[:2]
    nb = T // BLK
    pad = [(0, 0), (BLK, BLK), (0, 0), (0, 0)]
    kp = jnp.pad(k, pad)
    vp = jnp.pad(v, pad)
    r = jnp.arange(BLK)
    rk = jnp.arange(3 * BLK)
    ctx_mask = jnp.ones((BLK, kc.shape[1]), dtype=bool)

    def block(i):
        start = i * BLK
        qi = lax.dynamic_slice_in_dim(q, start, BLK, axis=1)
        ki = lax.dynamic_slice_in_dim(kp, start, 3 * BLK, axis=1)
        vi = lax.dynamic_slice_in_dim(vp, start, 3 * BLK, axis=1)
        qpos = start + r
        kpos = start - BLK + rk
        band = (jnp.abs(qpos[:, None] - kpos[None, :]) <= WINDOW) & ((kpos >= 0) & (kpos < T))[None, :]
        mask = jnp.concatenate([ctx_mask, band], axis=1)
        return attend(qi, jnp.concatenate([kc, ki], 1), jnp.concatenate([vc, vi], 1), mask, sink)

    out = lax.map(block, jnp.arange(nb))
    return jnp.moveaxis(out, 0, 1).reshape((B, T) + out.shape[3:])


def conv_module(u, w_dw, b_dw, ln_g, ln_b):
    a, gt = jnp.split(u, 2, axis=-1)
    g = a * jax.nn.sigmoid(gt)
    y = lax.conv_general_dilated(g, w_dw[:, None, :].astype(g.dtype), window_strides=(1,),
                                 padding=[(CONV_K // 2, CONV_K // 2)],
                                 dimension_numbers=('NWC', 'WIO', 'NWC'),
                                 feature_group_count=CONV_CH) + b_dw
    return jax.nn.silu(layernorm(y, ln_g, ln_b))


def mla_kv(ckv_n, kr, w_ukv, kn_g):
    B, T, _ = ckv_n.shape
    kv = (ckv_n @ w_ukv).reshape(B, T, MLA_H, MLA_NOPE + MLA_V)
    k_nope, v = kv[..., :MLA_NOPE], kv[..., MLA_NOPE:]
    k = jnp.concatenate([k_nope, jnp.broadcast_to(kr[:, :, None, :], (B, T, MLA_H, MLA_ROPE))], -1)
    return rmsnorm(k, kn_g), v


def mix_block(h, lw, ctx, tabs):
    B, T, _ = h.shape
    u_a, q_b, k_b, v_b, q_c, k_c, v_c, cq, ckv, kr = split_cols(h @ lw['w_in'])
    y_a = conv_module(u_a, lw['conv_w'], lw['conv_b'], lw['conv_ln_g'], lw['conv_ln_b'])
    q_b = rmsnorm(q_b.reshape(B, T, WIN_H, HD), lw['win_qn'])
    k_b = rmsnorm(k_b.reshape(B, T, WIN_KV, HD), lw['win_kn'])
    v_b = v_b.reshape(B, T, WIN_KV, HD)
    sink = lw['win_sink'].reshape(WIN_KV, WIN_H // WIN_KV)
    q_c = rmsnorm(q_c.reshape(B, T, GLOB_H, HD), lw['glob_qn'])
    k_c = rmsnorm(k_c.reshape(B, T, GLOB_KV, HD), lw['glob_kn'])
    v_c = v_c.reshape(B, T, GLOB_KV, HD)
    ckv = rmsnorm(ckv, lw['mla_kv_norm'])
    q_d = rmsnorm((rmsnorm(cq, lw['mla_qa_norm']) @ lw['mla_w_uq']).reshape(B, T, MLA_H, MLA_QK), lw['mla_qn'])
    k_d, v_d = mla_kv(ckv, kr, lw['mla_w_ukv'], lw['mla_kn'])
    if ctx is None:
        o_b = dense_attention(group_heads(q_b, WIN_KV), k_b, v_b, sink)
        o_c = dense_attention(group_heads(q_c, GLOB_KV), k_c, v_c)
        o_d = dense_attention(q_d[:, :, :, None, :], k_d, v_d)
        state = (k_b, v_b, k_c, v_c, ckv, kr)
    else:
        kb_x, vb_x, kc_x, vc_x, ckv_x, kr_x = ctx
        (cos_h, sin_h), (cos_r, sin_r) = tabs
        q_b = apply_rope(q_b, cos_h, sin_h)
        k_b = apply_rope(k_b, cos_h, sin_h)
        o_b = banded_attention(group_heads(q_b, WIN_KV), k_b, v_b, kb_x, vb_x, sink)
        q_c = apply_rope(q_c, cos_h, sin_h)
        k_c = apply_rope(k_c, cos_h, sin_h)
        o_c = dense_attention(group_heads(q_c, GLOB_KV), jnp.concatenate([kc_x, k_c], 1),
                              jnp.concatenate([vc_x, v_c], 1))
        q_d = rope_tail(q_d, cos_r, sin_r)
        k_d = rope_tail(k_d, cos_r, sin_r)
        kd_x, vd_x = mla_kv(ckv_x, kr_x, lw['mla_w_ukv'], lw['mla_kn'])
        o_d = dense_attention(q_d[:, :, :, None, :], jnp.concatenate([kd_x, k_d], 1),
                              jnp.concatenate([vd_x, v_d], 1))
        state = None
    mixed = jnp.concatenate([y_a, o_b.reshape(B, T, -1), o_c.reshape(B, T, -1), o_d.reshape(B, T, -1)], -1)
    return mixed @ lw['w_out'], state


def peer(h, wq, keys, U, V):
    B, T, D = h.shape
    n = B * T
    x = h.reshape(n, D)
    q = (x @ wq).reshape(n, PEER_HEADS, 2, PEER_DQ // 2)
    s = jnp.einsum('nhpd,hpkd->nhpk', q, keys, preferred_element_type=jnp.float32)
    v1, i1 = lax.top_k(s[:, :, 0], PEER_TOPK)
    v2, i2 = lax.top_k(s[:, :, 1], PEER_TOPK)
    cand = (v1[..., :, None] + v2[..., None, :]).reshape(n, PEER_HEADS, PEER_TOPK * PEER_TOPK)
    vs, flat = lax.top_k(cand, PEER_TOPK)
    e = (jnp.take_along_axis(i1, flat // PEER_TOPK, -1) * N_KEYS
         + jnp.take_along_axis(i2, flat % PEER_TOPK, -1))
    g = jax.nn.softmax(vs, axis=-1)
    nb = n // PEER_TBLK
    e = e.reshape(nb, PEER_TBLK, PEER_HEADS * PEER_TOPK)
    g = g.reshape(nb, PEER_TBLK, PEER_HEADS * PEER_TOPK)

    def blk(args):
        xb, eb, gb = args
        a = jax.nn.gelu(jnp.einsum('td,tkd->tk', xb, U[eb], preferred_element_type=jnp.float32))
        w = (a * gb).astype(xb.dtype)
        return jnp.einsum('tk,tkd->td', w, V[eb])

    out = lax.map(blk, (x.reshape(nb, PEER_TBLK, D), e, g))
    return out.reshape(B, T, D)


def modulation(cond, w, b):
    return jnp.split(jax.nn.silu(cond) @ w + b, 6, axis=-1)


def trunk_layer(x, mods, lw, ctx, tabs):
    sh1, sc1, g1, sh2, sc2, g2 = mods
    h = rmsnorm(x, lw['norm1']) * (1 + sc1) + sh1
    y, state = mix_block(h, lw, ctx, tabs)
    x = x + g1 * y
    h = rmsnorm(x, lw['norm2']) * (1 + sc2) + sh2
    x = x + g2 * peer(h, lw['peer_wq'], lw['peer_keys'], lw['peer_u'], lw['peer_v'])
    return x, state


def setup_inputs(seed: int = 0) -> dict:
    key = jax.random.key(seed)
    ks = iter(jax.random.split(key, 40))

    def nrm(shape, scale=1.0):
        return jax.random.normal(next(ks), shape, jnp.float32) * scale

    def gain(shape):
        return 1.0 + nrm(shape, 0.05)

    L = DEPTH
    return {
        'x_prompt': nrm((BATCH, SEQ, D_MODEL)),
        'x_sample': nrm((DEC_BATCH, DEC_SEQ, D_MODEL)),
        'cache_win_k': nrm((DEC_BATCH, L, PAST_LEN, WIN_KV, HD)),
        'cache_win_v': nrm((DEC_BATCH, L, PAST_LEN, WIN_KV, HD)),
        'cache_glob_k': nrm((DEC_BATCH, L, PAST_LEN, GLOB_KV, HD)),
        'cache_glob_v': nrm((DEC_BATCH, L, PAST_LEN, GLOB_KV, HD)),
        'cache_mla_ckv': nrm((DEC_BATCH, L, PAST_LEN, KV_LORA)),
        'cache_mla_krope': nrm((DEC_BATCH, L, PAST_LEN, MLA_ROPE)),
        'c': nrm((DEC_BATCH, D_MODEL)),
        'c_ctx': nrm((D_MODEL,)),
        'mod_w': nrm((L, D_MODEL, 6 * D_MODEL), 0.5 * D_MODEL ** -0.5),
        'mod_b': nrm((L, 6 * D_MODEL), 0.01),
        'norm1_g': gain((L, D_MODEL)),
        'norm2_g': gain((L, D_MODEL)),
        'w_in': nrm((L, D_MODEL, IN_COLS), D_MODEL ** -0.5),
        'conv_w': nrm((L, CONV_K, CONV_CH), CONV_K ** -0.5),
        'conv_b': nrm((L, CONV_CH), 0.02),
        'conv_ln_g': gain((L, CONV_CH)),
        'conv_ln_b': nrm((L, CONV_CH), 0.02),
        'win_qn_g': gain((L, HD)),
        'win_kn_g': gain((L, HD)),
        'win_sink': nrm((L, WIN_H), 0.5),
        'glob_qn_g': gain((L, HD)),
        'glob_kn_g': gain((L, HD)),
        'mla_qa_norm_g': gain((L, Q_LORA)),
        'mla_w_uq': nrm((L, Q_LORA, MLA_H * MLA_QK), Q_LORA ** -0.5),
        'mla_kv_norm_g': gain((L, KV_LORA)),
        'mla_w_ukv': nrm((L, KV_LORA, MLA_H * (MLA_NOPE + MLA_V)), KV_LORA ** -0.5),
        'mla_qn_g': gain((L, MLA_QK)),
        'mla_kn_g': gain((L, MLA_QK)),
        'w_out': nrm((L, MIX, D_MODEL), MIX ** -0.5),
        'peer_wq': nrm((L, D_MODEL, PEER_HEADS * PEER_DQ), D_MODEL ** -0.5),
        'peer_keys': nrm((L, PEER_HEADS, 2, N_KEYS, PEER_DQ // 2), (PEER_DQ // 2) ** -0.5),
        'peer_u': nrm((L, N_EXPERTS, D_MODEL), D_MODEL ** -0.5),
        'peer_v': nrm((L, N_EXPERTS, D_MODEL), PEER_HEADS ** -0.5),
    }


def reference(x_prompt, x_sample, cache_win_k, cache_win_v, cache_glob_k, cache_glob_v, cache_mla_ckv,
              cache_mla_krope, c, c_ctx, mod_w, mod_b, norm1_g, norm2_g, w_in, conv_w, conv_b, conv_ln_g,
              conv_ln_b, win_qn_g, win_kn_g, win_sink, glob_qn_g, glob_kn_g, mla_qa_norm_g, mla_w_uq,
              mla_kv_norm_g, mla_w_ukv, mla_qn_g, mla_kn_g, w_out, peer_wq, peer_keys, peer_u, peer_v):
    rows = x_sample.shape[1] // GRID_W
    tabs = (rope_tables(rows, HD), rope_tables(rows, MLA_ROPE))
    yp, ys = x_prompt, x_sample
    states = []
    for l in range(DEPTH):
        lw = dict(norm1=norm1_g[l], norm2=norm2_g[l], w_in=w_in[l], conv_w=conv_w[l], conv_b=conv_b[l],
                  conv_ln_g=conv_ln_g[l], conv_ln_b=conv_ln_b[l], win_qn=win_qn_g[l], win_kn=win_kn_g[l],
                  win_sink=win_sink[l], glob_qn=glob_qn_g[l], glob_kn=glob_kn_g[l],
                  mla_qa_norm=mla_qa_norm_g[l], mla_w_uq=mla_w_uq[l], mla_kv_norm=mla_kv_norm_g[l],
                  mla_w_ukv=mla_w_ukv[l], mla_qn=mla_qn_g[l], mla_kn=mla_kn_g[l], w_out=w_out[l],
                  peer_wq=peer_wq[l], peer_keys=peer_keys[l], peer_u=peer_u[l], peer_v=peer_v[l])
        mods_ctx = modulation(c_ctx, mod_w[l], mod_b[l])
        mods_lat = [m[:, None, :] for m in modulation(c, mod_w[l], mod_b[l])]
        ctx_l = (cache_win_k[:, l], cache_win_v[:, l], cache_glob_k[:, l], cache_glob_v[:, l],
                 cache_mla_ckv[:, l], cache_mla_krope[:, l])
        yp, st = trunk_layer(yp, mods_ctx, lw, None, None)
        ys, _ = trunk_layer(ys, mods_lat, lw, ctx_l, tabs)
        states.append(st)
    new_win_k = jnp.stack([s[0] for s in states], axis=1)
    new_win_v = jnp.stack([s[1] for s in states], axis=1)
    new_glob_k = jnp.stack([s[2] for s in states], axis=1)
    new_glob_v = jnp.stack([s[3] for s in states], axis=1)
    new_mla_ckv = jnp.stack([s[4] for s in states], axis=1)
    new_mla_krope = jnp.stack([s[5] for s in states], axis=1)
    return (yp, ys, new_win_k, new_win_v, new_glob_k, new_glob_v, new_mla_ckv, new_mla_krope)
```

```python
import functools

import jax
import jax.numpy as jnp
from jax import lax
from jax.experimental import pallas as pl
from jax.experimental.pallas import tpu as pltpu

F32 = jnp.float32
BF16 = jnp.bfloat16

HD = 128
CONV_K = 31
CONV_HALO = 16
WINDOW = 128
GRID_W = 64
EPS = 1e-6
NEG = -1e30
ROPE_THETA = 10000.0
WIN_KV = 2
GLOB_KV = 2
MLA_NOPE = 128
MLA_ROPE = 64
MLA_V = 128
MLA_QK = MLA_NOPE + MLA_ROPE
MLA_PAD = 256
PEER_HEADS = 8
N_KEYS = 128
PEER_TOPK = 16

LANES = 128
SUBLANES = 8
V7X_VMEM_BYTES = 64 * 2**20
VMEM_CAP = V7X_VMEM_BYTES - 6 * 2**20


def _cp(sem, block_bytes):
    limit = min(VMEM_CAP, max(32 * 2**20, int(2 * block_bytes * 1.5)))
    return pltpu.CompilerParams(dimension_semantics=sem, vmem_limit_bytes=limit)


def _nbytes(shape, dtype):
    n = 1
    for s in shape:
        n *= s
    return n * jnp.dtype(dtype).itemsize


def _pick_tile(N, cap):
    best = LANES
    for t in range(LANES, min(N, cap) + 1, LANES):
        if N % t == 0:
            best = t
    return best


def _mod_kernel(c_ref, w_ref, b_ref, o_ref):
    c = c_ref[...]
    s = (c * jax.nn.sigmoid(c)).astype(BF16)
    o_ref[0] = jnp.dot(s, w_ref[0].astype(BF16), preferred_element_type=F32) + b_ref[0]


def modulation(cond, mod_w, mod_b):
    L, D, N = mod_w.shape
    R = cond.shape[0]
    tn = 512
    blk = _nbytes((D, tn), F32) + _nbytes((D, tn), BF16)
    return pl.pallas_call(
        _mod_kernel,
        grid=(L, N // tn),
        in_specs=[pl.BlockSpec((R, D), lambda l, j: (0, 0)),
                  pl.BlockSpec((1, D, tn), lambda l, j: (l, 0, j)),
                  pl.BlockSpec((1, 1, tn), lambda l, j: (l, 0, j))],
        out_specs=pl.BlockSpec((1, R, tn), lambda l, j: (l, 0, j)),
        out_shape=jax.ShapeDtypeStruct((L, R, N), F32),
        compiler_params=_cp(("parallel", "parallel"), blk),
        name="modulation",
    )(cond, mod_w, mod_b.reshape(L, 1, N))


def _normmod_kernel(x_ref, g_ref, sc_ref, sh_ref, o_ref):
    x = x_ref[...]
    r = lax.rsqrt(jnp.mean(x * x, axis=-1, keepdims=True) + EPS)
    y = x * r * g_ref[...]
    o_ref[...] = (y * (1.0 + sc_ref[0]) + sh_ref[0]).astype(o_ref.dtype)


def _row_group(i, tm, T, Bm):
    return (i * tm) // T if Bm > 1 else 0


def norm_modulate(x, gain, scale, shift, T):
    n, D = x.shape
    Bm = scale.shape[0]
    tm = min(256, T)
    gmap = lambda i: (_row_group(i, tm, T, Bm), 0, 0)
    return pl.pallas_call(
        _normmod_kernel,
        grid=(n // tm,),
        in_specs=[pl.BlockSpec((tm, D), lambda i: (i, 0)),
                  pl.BlockSpec((1, D), lambda i: (0, 0)),
                  pl.BlockSpec((1, 1, D), gmap),
                  pl.BlockSpec((1, 1, D), gmap)],
        out_specs=pl.BlockSpec((tm, D), lambda i: (i, 0)),
        out_shape=jax.ShapeDtypeStruct((n, D), BF16),
        compiler_params=_cp(("parallel",), _nbytes((tm, D), F32) * 2),
        name="norm_modulate",
    )(x, gain.reshape(1, D), scale.reshape(Bm, 1, D), shift.reshape(Bm, 1, D))


def _mm_kernel(*refs, nx, norm, resid):
    xs = refs[:nx]
    w_ref = refs[nx]
    pos = nx + 1
    if norm:
        g_ref = refs[pos]
        pos += 1
    if resid:
        r_ref, gate_ref = refs[pos], refs[pos + 1]
        pos += 2
    o_ref = refs[pos]
    acc = None
    k0 = 0
    for x_ref in xs:
        x = x_ref[...]
        if norm:
            r = lax.rsqrt(jnp.mean(x * x, axis=-1, keepdims=True) + EPS)
            x = (x * r * g_ref[...]).astype(BF16)
        kw = x.shape[1]
        d = jnp.dot(x, w_ref[k0:k0 + kw, :], preferred_element_type=F32)
        acc = d if acc is None else acc + d
        k0 += kw
    if resid:
        acc = r_ref[...] + gate_ref[0] * acc
    o_ref[...] = acc.astype(o_ref.dtype)


def matmul(xs, w, *, x_cols=None, tm, tn, out_dtype=F32, norm_gain=None, resid=None, gate=None, T=None,
           name="matmul"):
    if not isinstance(xs, (list, tuple)):
        xs = [xs]
    M = xs[0].shape[0]
    K, N = w.shape
    if x_cols is None:
        x_cols = [(0, x.shape[1]) for x in xs]
    assert sum(wd for _, wd in x_cols) == K and M % tm == 0 and N % tn == 0
    in_specs = [pl.BlockSpec((tm, wd), functools.partial(lambda i, j, cb: (i, cb), cb=cb)) for cb, wd in x_cols]
    in_specs.append(pl.BlockSpec((K, tn), lambda i, j: (0, j)))
    args = list(xs) + [w]
    if norm_gain is not None:
        assert len(xs) == 1
        in_specs.append(pl.BlockSpec((1, K), lambda i, j: (0, 0)))
        args.append(norm_gain.reshape(1, K))
    blk = sum(_nbytes((tm, wd), x.dtype) for x, (_, wd) in zip(xs, x_cols)) + _nbytes((K, tn), w.dtype) \
        + _nbytes((tm, tn), F32) * 2
    if resid is not None:
        Bm = gate.shape[0]
        in_specs.append(pl.BlockSpec((tm, tn), lambda i, j: (i, j)))
        in_specs.append(pl.BlockSpec((1, 1, tn), lambda i, j: (_row_group(i, tm, T, Bm), 0, j)))
        args += [resid, gate.reshape(Bm, 1, N)]
        blk += _nbytes((tm, tn), F32)
    return pl.pallas_call(
        functools.partial(_mm_kernel, nx=len(xs), norm=norm_gain is not None, resid=resid is not None),
        grid=(M // tm, N // tn),
        in_specs=in_specs,
        out_specs=pl.BlockSpec((tm, tn), lambda i, j: (i, j)),
        out_shape=jax.ShapeDtypeStruct((M, N), out_dtype),
        compiler_params=_cp(("parallel", "parallel"), blk),
        name=name,
    )(*args)


def _conv_kernel(a_ref, g_ref, ap_ref, gp_ref, an_ref, gn_ref, w_ref, b_ref, lg_ref, lb_ref, o_ref,
                 buf_ref, y_ref, *, tm, T, C):
    i = pl.program_id(0)
    pos = lax.rem(i * tm, T)
    has_prev = pos != 0
    has_next = pos + tm != T
    H = CONV_HALO
    buf_ref[H:H + tm, :] = a_ref[...] * jax.nn.sigmoid(g_ref[...])
    buf_ref[0:H, :] = jnp.where(has_prev, ap_ref[...] * jax.nn.sigmoid(gp_ref[...]), 0.0)
    buf_ref[H + tm:2 * H + tm, :] = jnp.where(has_next, an_ref[...] * jax.nn.sigmoid(gn_ref[...]), 0.0)
    half = CONV_K // 2
    for c in range(C // LANES):
        cs = slice(c * LANES, (c + 1) * LANES)
        acc = jnp.zeros((tm, LANES), F32)
        for k in range(CONV_K):
            acc = acc + w_ref[k:k + 1, cs] * buf_ref[pl.ds(H - half + k, tm), cs]
        y_ref[:, cs] = acc + b_ref[:, cs]
    y = y_ref[...]
    mu = jnp.mean(y, axis=-1, keepdims=True)
    d = y - mu
    var = jnp.mean(d * d, axis=-1, keepdims=True)
    z = d * lax.rsqrt(var + EPS) * lg_ref[...] + lb_ref[...]
    o_ref[...] = (z * jax.nn.sigmoid(z)).astype(o_ref.dtype)


def conv_module(p, cb, w_dw, b_dw, ln_g, ln_b, T, C):
    n = p.shape[0]
    tm = min(128, T)
    H = CONV_HALO
    nb = n // H
    wpad = jnp.zeros((32, C), F32).at[:CONV_K].set(w_dw)
    prev = lambda i, cb: (jnp.maximum(i * (tm // H) - 1, 0), cb)
    nxt = lambda i, cb: (jnp.minimum((i + 1) * (tm // H), nb - 1), cb)
    vec = lambda v: v.reshape(1, C)
    full = pl.BlockSpec((1, C), lambda i: (0, 0))
    return pl.pallas_call(
        functools.partial(_conv_kernel, tm=tm, T=T, C=C),
        grid=(n // tm,),
        in_specs=[pl.BlockSpec((tm, C), lambda i: (i, cb)),
                  pl.BlockSpec((tm, C), lambda i: (i, cb + 1)),
                  pl.BlockSpec((H, C), lambda i: prev(i, cb)),
                  pl.BlockSpec((H, C), lambda i: prev(i, cb + 1)),
                  pl.BlockSpec((H, C), lambda i: nxt(i, cb)),
                  pl.BlockSpec((H, C), lambda i: nxt(i, cb + 1)),
                  pl.BlockSpec((32, C), lambda i: (0, 0)),
                  full, full, full],
        out_specs=pl.BlockSpec((tm, C), lambda i: (i, 0)),
        out_shape=jax.ShapeDtypeStruct((n, C), BF16),
        scratch_shapes=[pltpu.VMEM((tm + 2 * H, C), F32), pltpu.VMEM((tm, C), F32)],
        compiler_params=_cp(("parallel",), _nbytes((tm + 2 * H, C), F32) * 4),
        name="conv_module",
    )(p, p, p, p, p, p, wpad, vec(b_dw), vec(ln_g), vec(ln_b))


def _rope_angles(T, dim):
    half = dim // 2
    nf = half // 2
    inv = ROPE_THETA ** (-(jnp.arange(nf, dtype=F32) * (2.0 / half)))
    rows = T // GRID_W
    row = jnp.repeat(jnp.arange(rows, dtype=F32), GRID_W)
    col = jnp.tile(jnp.arange(GRID_W, dtype=F32), rows)
    ang = jnp.concatenate([row[:, None] * inv, col[:, None] * inv], -1)
    return jnp.cos(ang), jnp.sin(ang)


def rope_tables_head(T):
    c, s = _rope_angles(T, HD)
    return jnp.concatenate([c, c], -1), jnp.concatenate([-s, s], -1)


def rope_tables_tail(T):
    c, s = _rope_angles(T, MLA_ROPE)
    one = jnp.ones((T, LANES - MLA_ROPE), F32)
    z32 = jnp.zeros((T, MLA_ROPE // 2), F32)
    z64 = jnp.zeros((T, LANES - MLA_ROPE), F32)
    cos = jnp.concatenate([c, c, one], -1)
    s_lo = jnp.concatenate([z32, s, z64], -1)
    s_hi = jnp.concatenate([-s, z32, z64], -1)
    return cos, s_lo, s_hi


def _qkv_prep_kernel(*refs, kinds, rope, state_heads):
    x_ref, g_ref = refs[0], refs[1]
    pos = 2
    if rope:
        cos_ref, sin_ref = refs[2], refs[3]
        pos = 4
    o_ref = refs[pos]
    st_ref = refs[pos + 1] if state_heads else None
    if rope:
        cos, sin = cos_ref[...], sin_ref[...]
    for c, kind in enumerate(kinds):
        cs = slice(c * HD, (c + 1) * HD)
        x = x_ref[:, cs]
        if kind >= 0:
            r = lax.rsqrt(jnp.mean(x * x, axis=-1, keepdims=True) + EPS)
            y = x * r * g_ref[kind:kind + 1, :]
            if state_heads and c in state_heads:
                k = state_heads.index(c)
                st_ref[:, k * HD:(k + 1) * HD] = y
            if rope:
                y = y * cos + pltpu.roll(y, HD // 2, 1) * sin
        else:
            y = x
        o_ref[:, cs] = y.astype(o_ref.dtype)


def qkv_prep(p, col0, kinds, gains, T, rope_tabs=None, state_heads=()):
    n = p.shape[0]
    W = HD * len(kinds)
    assert col0 % W == 0
    tm = min(256, T)
    in_specs = [pl.BlockSpec((tm, W), lambda i: (i, col0 // W)),
                pl.BlockSpec(gains.shape, lambda i: (0, 0))]
    args = [p, gains]
    if rope_tabs is not None:
        nt = T // tm
        in_specs += [pl.BlockSpec((tm, HD), lambda i: (i % nt, 0))] * 2
        args += list(rope_tabs)
    out_shape = [jax.ShapeDtypeStruct((n, W), BF16)]
    out_specs = [pl.BlockSpec((tm, W), lambda i: (i, 0))]
    if state_heads:
        out_shape.append(jax.ShapeDtypeStruct((n, HD * len(state_heads)), F32))
        out_specs.append(pl.BlockSpec((tm, HD * len(state_heads)), lambda i: (i, 0)))
    out = pl.pallas_call(
        functools.partial(_qkv_prep_kernel, kinds=tuple(kinds), rope=rope_tabs is not None,
                          state_heads=tuple(state_heads)),
        grid=(n // tm,),
        in_specs=in_specs,
        out_specs=out_specs,
        out_shape=out_shape,
        compiler_params=_cp(("parallel",), _nbytes((tm, W), F32) * 2),
        name="qkv_prep",
    )(*args)
    return out if state_heads else out[0]


def _rope_tail(y2, tabs):
    cos, s_lo, s_hi = tabs
    r = MLA_ROPE // 2
    return y2 * cos + pltpu.roll(y2, r, 1) * s_lo + pltpu.roll(y2, LANES - r, 1) * s_hi


def _mla_q_prep_kernel(*refs, H, rope):
    x_ref, g_ref = refs[0], refs[1]
    tabs = tuple(t[...] for t in refs[2:5]) if rope else None
    o_ref = refs[5] if rope else refs[2]
    for h in range(H):
        lo = slice(h * MLA_PAD, h * MLA_PAD + LANES)
        hi = slice(h * MLA_PAD + LANES, (h + 1) * MLA_PAD)
        x1, x2 = x_ref[:, lo], x_ref[:, hi]
        ms = (jnp.sum(x1 * x1, axis=-1, keepdims=True) + jnp.sum(x2 * x2, axis=-1, keepdims=True)) * (1.0 / MLA_QK)
        r = lax.rsqrt(ms + EPS)
        y1 = x1 * r * g_ref[:, 0:LANES]
        y2 = x2 * r * g_ref[:, LANES:MLA_PAD]
        if rope:
            y2 = _rope_tail(y2, tabs)
        o_ref[:, lo] = y1.astype(o_ref.dtype)
        o_ref[:, hi] = y2.astype(o_ref.dtype)


def mla_q_prep(qraw, gain_pad, T, rope_tabs=None):
    n, W = qraw.shape
    H = W // MLA_PAD
    tm = min(256, T)
    in_specs = [pl.BlockSpec((tm, W), lambda i: (i, 0)), pl.BlockSpec((1, MLA_PAD), lambda i: (0, 0))]
    args = [qraw, gain_pad.reshape(1, MLA_PAD)]
    if rope_tabs is not None:
        nt = T // tm
        in_specs += [pl.BlockSpec((tm, LANES), lambda i: (i % nt, 0))] * 3
        args += list(rope_tabs)
    return pl.pallas_call(
        functools.partial(_mla_q_prep_kernel, H=H, rope=rope_tabs is not None),
        grid=(n // tm,),
        in_specs=in_specs,
        out_specs=pl.BlockSpec((tm, W), lambda i: (i, 0)),
        out_shape=jax.ShapeDtypeStruct((n, W), BF16),
        compiler_params=_cp(("parallel",), _nbytes((tm, W), F32) * 2),
        name="mla_q_prep",
    )(*args)


def _mla_k_prep_kernel(*refs, H, rope):
    kv_ref, kr_ref, g_ref = refs[0], refs[1], refs[2]
    tabs = tuple(t[...] for t in refs[3:6]) if rope else None
    k_ref, v_ref = (refs[6], refs[7]) if rope else (refs[3], refs[4])
    x2 = kr_ref[...]
    s2 = jnp.sum(x2 * x2, axis=-1, keepdims=True)
    for h in range(H):
        x1 = kv_ref[:, h * 2 * LANES:h * 2 * LANES + LANES]
        ms = (jnp.sum(x1 * x1, axis=-1, keepdims=True) + s2) * (1.0 / MLA_QK)
        r = lax.rsqrt(ms + EPS)
        y1 = x1 * r * g_ref[:, 0:LANES]
        y2 = x2 * r * g_ref[:, LANES:MLA_PAD]
        if rope:
            y2 = _rope_tail(y2, tabs)
        k_ref[:, h * MLA_PAD:h * MLA_PAD + LANES] = y1.astype(k_ref.dtype)
        k_ref[:, h * MLA_PAD + LANES:(h + 1) * MLA_PAD] = y2.astype(k_ref.dtype)
        v_ref[:, h * LANES:(h + 1) * LANES] = kv_ref[:, h * 2 * LANES + LANES:(h + 1) * 2 * LANES].astype(v_ref.dtype)


def mla_k_prep(kvup, kr_arr, kr_cb, gain_pad, T, rope_tabs=None):
    n, W = kvup.shape
    H = W // (2 * LANES)
    tm = min(256, T)
    in_specs = [pl.BlockSpec((tm, W), lambda i: (i, 0)),
                pl.BlockSpec((tm, LANES), lambda i: (i, kr_cb)),
                pl.BlockSpec((1, MLA_PAD), lambda i: (0, 0))]
    args = [kvup, kr_arr, gain_pad.reshape(1, MLA_PAD)]
    if rope_tabs is not None:
        nt = T // tm
        in_specs += [pl.BlockSpec((tm, LANES), lambda i: (i % nt, 0))] * 3
        args += list(rope_tabs)
    return pl.pallas_call(
        functools.partial(_mla_k_prep_kernel, H=H, rope=rope_tabs is not None),
        grid=(n // tm,),
        in_specs=in_specs,
        out_specs=[pl.BlockSpec((tm, H * MLA_PAD), lambda i: (i, 0)),
                   pl.BlockSpec((tm, H * LANES), lambda i: (i, 0))],
        out_shape=[jax.ShapeDtypeStruct((n, H * MLA_PAD), BF16), jax.ShapeDtypeStruct((n, H * LANES), BF16)],
        compiler_params=_cp(("parallel",), _nbytes((tm, W), F32) * 2),
        name="mla_k_prep",
    )(*args)


def _rmsnorm_kernel(x_ref, g_ref, o_ref):
    x = x_ref[...]
    r = lax.rsqrt(jnp.mean(x * x, axis=-1, keepdims=True) + EPS)
    o_ref[...] = x * r * g_ref[...]


def rmsnorm_cols(p, cb, W, gain):
    n = p.shape[0]
    tm = 256
    return pl.pallas_call(
        _rmsnorm_kernel,
        grid=(n // tm,),
        in_specs=[pl.BlockSpec((tm, W), lambda i: (i, cb)), pl.BlockSpec((1, W), lambda i: (0, 0))],
        out_specs=pl.BlockSpec((tm, W), lambda i: (i, 0)),
        out_shape=jax.ShapeDtypeStruct((n, W), F32),
        compiler_params=_cp(("parallel",), _nbytes((tm, W), F32) * 2),
        name="rmsnorm_cols",
    )(p, gain.reshape(1, W))


def _stack_heads(q_ref, G, dk):
    if G == 1:
        return q_ref[...]
    return jnp.concatenate([q_ref[:, g * dk:(g + 1) * dk] for g in range(G)], axis=0)


def _sink_column(sink_ref, kvh, G, tq):
    cols = [jnp.full((tq, 1), sink_ref[kvh * G + g], F32) for g in range(G)]
    return cols[0] if G == 1 else jnp.concatenate(cols, axis=0)


_NT = (((1,), (1,)), ((), ()))


def _attn_kernel(*refs, G, dk, dv, tq, scale, has_ctx, has_sink):
    pos = 0
    if has_sink:
        sink_ref = refs[0]
        pos = 1
    q_ref, k_ref, v_ref = refs[pos:pos + 3]
    pos += 3
    if has_ctx:
        kc_ref, vc_ref = refs[pos:pos + 2]
        pos += 2
    o_ref = refs[pos]
    q = _stack_heads(q_ref, G, dk)
    s = lax.dot_general(q, k_ref[...], _NT, preferred_element_type=F32) * scale
    m = jnp.max(s, axis=-1, keepdims=True)
    if has_ctx:
        sc = lax.dot_general(q, kc_ref[...], _NT, preferred_element_type=F32) * scale
        m = jnp.maximum(m, jnp.max(sc, axis=-1, keepdims=True))
    if has_sink:
        sk = _sink_column(sink_ref, pl.program_id(1), G, tq)
        m = jnp.maximum(m, sk)
    p = jnp.exp(s - m)
    l = jnp.sum(p, axis=-1, keepdims=True)
    o = jnp.dot(p.astype(BF16), v_ref[...], preferred_element_type=F32)
    if has_ctx:
        pc = jnp.exp(sc - m)
        l = l + jnp.sum(pc, axis=-1, keepdims=True)
        o = o + jnp.dot(pc.astype(BF16), vc_ref[...], preferred_element_type=F32)
    if has_sink:
        l = l + jnp.exp(sk - m)
    o = o / l
    for g in range(G):
        o_ref[:, g * dv:(g + 1) * dv] = o[g * tq:(g + 1) * tq, :].astype(o_ref.dtype)


def attention(q_arr, q_cb0, k_arr, k_cb0, v_arr, v_cb0, *, B, T, KVH, G, dk, dv, tq, scale,
              ctx=None, sink=None, name="attention"):
    nt = T // tq
    has_ctx, has_sink = ctx is not None, sink is not None

    def im(f):
        return (lambda b, h, t, s: f(b, h, t)) if has_sink else f

    in_specs = [pl.BlockSpec((tq, G * dk), im(lambda b, h, t: (b * nt + t, q_cb0 + h))),
                pl.BlockSpec((T, dk), im(lambda b, h, t: (b, k_cb0 + h))),
                pl.BlockSpec((T, dv), im(lambda b, h, t: (b, v_cb0 + h)))]
    args = [q_arr, k_arr, v_arr]
    blk = _nbytes((G * tq, T), F32) * 3
    if has_ctx:
        kc, vc, Tc = ctx
        in_specs += [pl.BlockSpec((Tc, dk), im(lambda b, h, t: (b, h))),
                     pl.BlockSpec((Tc, dv), im(lambda b, h, t: (b, h)))]
        args += [kc, vc]
    kern = functools.partial(_attn_kernel, G=G, dk=dk, dv=dv, tq=tq, scale=scale, has_ctx=has_ctx, has_sink=has_sink)
    out_spec = pl.BlockSpec((tq, G * dv), im(lambda b, h, t: (b * nt + t, h)))
    out_shape = jax.ShapeDtypeStruct((B * T, KVH * G * dv), BF16)
    cparams = _cp(("parallel", "parallel", "parallel"), blk)
    if has_sink:
        gs = pltpu.PrefetchScalarGridSpec(num_scalar_prefetch=1, grid=(B, KVH, nt), in_specs=in_specs, out_specs=out_spec)
        return pl.pallas_call(kern, grid_spec=gs, out_shape=out_shape, compiler_params=cparams, name=name)(sink, *args)
    return pl.pallas_call(kern, grid=(B, KVH, nt), in_specs=in_specs, out_specs=out_spec, out_shape=out_shape,
                          compiler_params=cparams, name=name)(*args)


def _win_attn_kernel(sink_ref, q_ref, kp_ref, k0_ref, kn_ref, vp_ref, v0_ref, vn_ref, kc_ref, vc_ref, o_ref,
                     *, G, nb, scale):
    i = pl.program_id(2)
    tq = WINDOW
    R = G * tq
    q = _stack_heads(q_ref, G, HD)
    kcat = jnp.concatenate([kp_ref[...], k0_ref[...], kn_ref[...]], axis=0)
    vcat = jnp.concatenate([vp_ref[...], v0_ref[...], vn_ref[...]], axis=0)
    s = lax.dot_general(q, kcat, _NT, preferred_element_type=F32) * scale
    sc = lax.dot_general(q, kc_ref[...], _NT, preferred_element_type=F32) * scale
    row = lax.broadcasted_iota(jnp.int32, (R, 3 * tq), 0) & (tq - 1)
    col = lax.broadcasted_iota(jnp.int32, (R, 3 * tq), 1)
    lo = jnp.where(i == 0, tq, 0)
    hi = jnp.where(i == nb - 1, 2 * tq - 1, 3 * tq - 1)
    vis = (col >= jnp.maximum(row, lo)) & (col <= jnp.minimum(row + 2 * WINDOW, hi))
    s = jnp.where(vis, s, NEG)
    sk = _sink_column(sink_ref, pl.program_id(1), G, tq)
    m = jnp.maximum(jnp.maximum(jnp.max(s, axis=-1, keepdims=True), jnp.max(sc, axis=-1, keepdims=True)), sk)
    p = jnp.exp(s - m)
    pc = jnp.exp(sc - m)
    l = jnp.sum(p, axis=-1, keepdims=True) + jnp.sum(pc, axis=-1, keepdims=True) + jnp.exp(sk - m)
    o = jnp.dot(p.astype(BF16), vcat, preferred_element_type=F32) \
        + jnp.dot(pc.astype(BF16), vc_ref[...], preferred_element_type=F32)
    o = o / l
    for g in range(G):
        o_ref[:, g * HD:(g + 1) * HD] = o[g * tq:(g + 1) * tq, :].astype(o_ref.dtype)


def window_attention(qkv, q_cb0, k_cb0, v_cb0, kc, vc, Tc, sink, *, B, T, KVH, G, scale):
    tq = WINDOW
    nb = T // tq
    prv = lambda b, h, t, s, cb: (b * nb + jnp.maximum(t - 1, 0), cb + h)
    cur = lambda b, h, t, s, cb: (b * nb + t, cb + h)
    nxt = lambda b, h, t, s, cb: (b * nb + jnp.minimum(t + 1, nb - 1), cb + h)
    kv = lambda f, cb: pl.BlockSpec((tq, HD), functools.partial(f, cb=cb))
    in_specs = [pl.BlockSpec((tq, G * HD), lambda b, h, t, s: (b * nb + t, q_cb0 + h)),
                kv(prv, k_cb0), kv(cur, k_cb0), kv(nxt, k_cb0),
                kv(prv, v_cb0), kv(cur, v_cb0), kv(nxt, v_cb0),
                pl.BlockSpec((Tc, HD), lambda b, h, t, s: (b, h)),
                pl.BlockSpec((Tc, HD), lambda b, h, t, s: (b, h))]
    gs = pltpu.PrefetchScalarGridSpec(
        num_scalar_prefetch=1, grid=(B, KVH, nb), in_specs=in_specs,
        out_specs=pl.BlockSpec((tq, G * HD), lambda b, h, t, s: (b * nb + t, h)))
    return pl.pallas_call(
        functools.partial(_win_attn_kernel, G=G, nb=nb, scale=scale),
        grid_spec=gs,
        out_shape=jax.ShapeDtypeStruct((B * T, KVH * G * HD), BF16),
        compiler_params=_cp(("parallel", "parallel", "parallel"), _nbytes((G * tq, 5 * tq), F32) * 3),
        name="window_attention",
    )(sink, qkv, qkv, qkv, qkv, qkv, qkv, qkv, kc, vc)


def _top16_rows(sT, kio, rio):
    tt = sT.shape[1]
    vacc = jnp.zeros((PEER_TOPK, tt), F32)
    iacc = jnp.zeros((PEER_TOPK, tt), jnp.int32)
    rows = []
    for r in range(PEER_TOPK):
        m = jnp.max(sT, axis=0, keepdims=True)
        idx = jnp.min(jnp.where(sT == m, kio, N_KEYS), axis=0, keepdims=True)
        sT = jnp.where(kio == idx, -jnp.inf, sT)
        vacc = jnp.where(rio == r, m, vacc)
        iacc = jnp.where(rio == r, idx, iacc)
        rows.append(m)
    return vacc, iacc, rows


def _peer_topk_kernel(q_ref, keys_ref, i1_ref, i2_ref, g_ref, b1_ref, b2_ref, bg_ref, *, tt):
    K = PEER_TOPK
    kio = lax.broadcasted_iota(jnp.int32, (N_KEYS, tt), 0)
    rio = lax.broadcasted_iota(jnp.int32, (K, tt), 0)
    rio8 = lax.broadcasted_iota(jnp.int32, (SUBLANES, tt), 0)
    ninf = -jnp.inf

    def head(h, carry):
        tops = []
        for p in range(2):
            col = pl.multiple_of((2 * h + p) * N_KEYS, N_KEYS)
            qb = q_ref[:, pl.ds(col, N_KEYS)].astype(BF16)
            sT = lax.dot_general(keys_ref[2 * h + p], qb, _NT, preferred_element_type=F32)
            tops.append(_top16_rows(sT, kio, rio))
        (v1, i1, r1), (v2, i2, r2) = tops
        xs = [r1[0] + v2[0:8], r1[0] + v2[8:16]]
        fs = [rio8, rio8 + 8]
        for a in range(1, 8):
            xs.append(jnp.where(rio8 < K // (a + 1), r1[a] + v2[0:8], ninf))
            fs.append(rio8 + a * K)
        xs.append(v1[8:16] + r2[0])
        fs.append((rio8 + 8) * K)
        big = K * K
        sel_v, sel_f = [], []
        for k in range(K):
            m = xs[0]
            for x in xs[1:]:
                m = jnp.maximum(m, x)
            m = jnp.max(m, axis=0, keepdims=True)
            f = jnp.where(xs[0] == m, fs[0], big)
            for x, fl in zip(xs[1:], fs[1:]):
                f = jnp.minimum(f, jnp.where(x == m, fl, big))
            f = jnp.min(f, axis=0, keepdims=True)
            xs = [jnp.where(fl == f, ninf, x) for x, fl in zip(xs, fs)]
            sel_v.append(m)
            sel_f.append(f)
        es = [jnp.exp(v - sel_v[0]) for v in sel_v]
        z = es[0]
        for e in es[1:]:
            z = z + e
        o1 = jnp.zeros((K, tt), F32)
        o2 = jnp.zeros((K, tt), F32)
        og = jnp.zeros((K, tt), F32)
        for k in range(K):
            a = sel_f[k] >> 4
            b = sel_f[k] & (K - 1)
            e1 = jnp.sum(jnp.where(rio == a, i1, 0), axis=0, keepdims=True).astype(F32)
            e2 = jnp.sum(jnp.where(rio == b, i2, 0), axis=0, keepdims=True).astype(F32)
            o1 = jnp.where(rio == k, e1, o1)
            o2 = jnp.where(rio == k, e2, o2)
            og = jnp.where(rio == k, es[k] / z, og)
        r0 = pl.multiple_of(h * K, K)
        b1_ref[pl.ds(r0, K), :] = o1
        b2_ref[pl.ds(r0, K), :] = o2
        bg_ref[pl.ds(r0, K), :] = og
        return carry

    lax.fori_loop(0, PEER_HEADS, head, 0)
    for c in range(tt // LANES):
        cs = slice(c * LANES, (c + 1) * LANES)
        i1_ref[cs, :] = b1_ref[:, cs].T
        i2_ref[cs, :] = b2_ref[:, cs].T
        g_ref[cs, :] = bg_ref[:, cs].T


def peer_topk(q, keys_bf16):
    n = q.shape[0]
    tt = 256
    J = PEER_HEADS * PEER_TOPK
    sds = jax.ShapeDtypeStruct((n, J), F32)
    ospec = pl.BlockSpec((tt, J), lambda i: (i, 0))
    return pl.pallas_call(
        functools.partial(_peer_topk_kernel, tt=tt),
        grid=(n // tt,),
        in_specs=[pl.BlockSpec((tt, q.shape[1]), lambda i: (i, 0)),
                  pl.BlockSpec(keys_bf16.shape, lambda i: (0, 0, 0))],
        out_specs=[ospec, ospec, ospec],
        out_shape=[sds, sds, sds],
        scratch_shapes=[pltpu.VMEM((J, tt), F32)] * 3,
        compiler_params=_cp(("parallel",), _nbytes((tt, q.shape[1]), F32) * 2),
        name="peer_topk",
    )(q, keys_bf16)


def _peer_gate_kernel(i1_ref, i2_ref, g_ref, o_ref, *, tg):
    aio = lax.broadcasted_iota(jnp.int32, (N_KEYS, LANES), 0).astype(F32)

    def group(q, carry):
        for u in range(SUBLANES):
            t = q * SUBLANES + u
            i1r = i1_ref[pl.ds(t, 1), :]
            i2r = i2_ref[pl.ds(t, 1), :]
            gr = g_ref[pl.ds(t, 1), :]
            pt = jnp.where(aio == i1r, gr, 0.0).astype(BF16)
            qt = jnp.where(aio == i2r, 1.0, 0.0).astype(BF16)
            r = lax.dot_general(pt, qt, _NT, preferred_element_type=F32)
            o_ref[q, pl.ds(u, N_KEYS, stride=SUBLANES), :] = r
        return carry

    lax.fori_loop(0, tg // SUBLANES, group, 0)


def peer_gate_grid(i1, i2, g):
    n, J = i1.shape
    tg = 64
    ispec = pl.BlockSpec((tg, J), lambda i: (i, 0))
    return pl.pallas_call(
        functools.partial(_peer_gate_kernel, tg=tg),
        grid=(n // tg,),
        in_specs=[ispec, ispec, ispec],
        out_specs=pl.BlockSpec((tg // SUBLANES, N_KEYS * SUBLANES, LANES), lambda i: (i, 0, 0)),
        out_shape=jax.ShapeDtypeStruct((n // SUBLANES, N_KEYS * SUBLANES, LANES), F32),
        compiler_params=_cp(("parallel",), _nbytes((tg, N_KEYS, LANES), F32)),
        name="peer_gate_grid",
    )(i1, i2, g)


def _gelu_tanh(x):
    return 0.5 * x * (1.0 + jnp.tanh(0.7978845608028654 * (x + 0.044715 * (x * x * x))))


def _peer_dense_kernel(x_ref, ut_ref, v_ref, g_ref, o_ref, *, tm, tn):
    @pl.when(pl.program_id(1) == 0)
    def _():
        o_ref[...] = jnp.zeros_like(o_ref)

    a = jnp.dot(x_ref[...], ut_ref[...], preferred_element_type=F32)
    hh = _gelu_tanh(a)
    parts = []
    for al in range(tn // LANES):
        gt = g_ref[:, al].reshape(tm, LANES)
        parts.append((hh[:, al * LANES:(al + 1) * LANES] * gt).astype(BF16))
    w = jnp.concatenate(parts, axis=1)
    o_ref[...] += jnp.dot(w, v_ref[...], preferred_element_type=F32)


def peer_dense(h, ut, v, gate_grid):
    n, D = h.shape
    E = v.shape[0]
    tm, tn = 512, 512
    na = tn // LANES
    g4 = gate_grid.reshape(n // SUBLANES, N_KEYS, SUBLANES, LANES)
    blk = _nbytes((tm, D), BF16) + 2 * _nbytes((tn, D), BF16) + _nbytes((tm, tn), F32) * 2 + _nbytes((tm, D), F32)
    return pl.pallas_call(
        functools.partial(_peer_dense_kernel, tm=tm, tn=tn),
        grid=(n // tm, E // tn),
        in_specs=[pl.BlockSpec((tm, D), lambda i, j: (i, 0)),
                  pl.BlockSpec((D, tn), lambda i, j: (0, j)),
                  pl.BlockSpec((tn, D), lambda i, j: (j, 0)),
                  pl.BlockSpec((tm // SUBLANES, na, SUBLANES, LANES), lambda i, j: (i, j, 0, 0))],
        out_specs=pl.BlockSpec((tm, D), lambda i, j: (i, 0)),
        out_shape=jax.ShapeDtypeStruct((n, D), F32),
        compiler_params=_cp(("parallel", "arbitrary"), blk),
        name="peer_dense",
    )(h, ut, v, g4)


def _gated_add_kernel(x_ref, y_ref, g_ref, o_ref):
    o_ref[...] = x_ref[...] + g_ref[0] * y_ref[...]


def gated_add(x, y, gate, T):
    n, D = x.shape
    Bm = gate.shape[0]
    tm = min(256, T)
    spec = pl.BlockSpec((tm, D), lambda i: (i, 0))
    return pl.pallas_call(
        _gated_add_kernel,
        grid=(n // tm,),
        in_specs=[spec, spec, pl.BlockSpec((1, 1, D), lambda i: (_row_group(i, tm, T, Bm), 0, 0))],
        out_specs=spec,
        out_shape=jax.ShapeDtypeStruct((n, D), F32),
        compiler_params=_cp(("parallel",), _nbytes((tm, D), F32) * 3),
        name="gated_add",
    )(x, y, gate.reshape(Bm, 1, D))


def _pad_heads(w, H, width, pad_to):
    K = w.shape[0]
    w = w.reshape(K, H, width)
    return jnp.pad(w, ((0, 0), (0, 0), (0, pad_to - width))).reshape(K, H * pad_to)


def _layer_weights(l, D, w_in, conv_w, conv_b, conv_ln_g, conv_ln_b, win_qn_g, win_kn_g, win_sink, glob_qn_g,
                   glob_kn_g, mla_qa_norm_g, mla_w_uq, mla_kv_norm_g, mla_w_ukv, mla_qn_g, mla_kn_g, w_out,
                   peer_wq, peer_keys, peer_u, peer_v, norm1_g, norm2_g):
    C = D // 4
    H = C // HD
    q_lora = mla_w_uq.shape[1]
    kv_lora = mla_w_ukv.shape[1]
    o_cq = 2 * C + 2 * (C + 2 * WIN_KV * HD)
    o_ckv = o_cq + q_lora
    o_kr = o_ckv + kv_lora
    wi = w_in[l]
    w_in_p = jnp.concatenate([wi[:, 2 * C:o_cq], wi[:, :2 * C], wi[:, o_ckv:o_kr], wi[:, o_cq:o_ckv], wi[:, o_kr:],
                              jnp.zeros((D, LANES - MLA_ROPE), F32)], axis=1).astype(BF16)
    pad_gain = lambda g: jnp.pad(g, (0, MLA_PAD - MLA_QK))
    return dict(
        C=C, H=H, q_lora=q_lora, kv_lora=kv_lora,
        norm1=norm1_g[l], norm2=norm2_g[l], w_in=w_in_p,
        conv_w=conv_w[l], conv_b=conv_b[l], conv_ln_g=conv_ln_g[l], conv_ln_b=conv_ln_b[l],
        gains=jnp.stack([win_qn_g[l], win_kn_g[l], glob_qn_g[l], glob_kn_g[l]]),
        win_sink=win_sink[l],
        qa_norm=mla_qa_norm_g[l], kv_norm=mla_kv_norm_g[l],
        w_uq=_pad_heads(mla_w_uq[l], H, MLA_QK, MLA_PAD).astype(BF16),
        w_ukv=mla_w_ukv[l].astype(BF16),
        qn=pad_gain(mla_qn_g[l]), kn=pad_gain(mla_kn_g[l]),
        w_out=w_out[l].astype(BF16),
        wq=peer_wq[l].astype(BF16),
        keys=peer_keys[l].reshape(2 * PEER_HEADS, N_KEYS, -1).astype(BF16),
        ut=peer_u[l].T.astype(BF16), v=peer_v[l].astype(BF16),
    )


def _mix_block(h, lw, B, T, ctx, tabs):
    n = h.shape[0]
    C, H = lw["C"], lw["H"]
    G = H // WIN_KV
    kvw = WIN_KV * HD
    p = matmul(h, lw["w_in"], tm=min(1024, n), tn=_pick_tile(lw["w_in"].shape[1], 1024), name="w_in")
    kinds = [0] * H + [1] * WIN_KV + [-1] * WIN_KV + [2] * H + [3] * GLOB_KV + [-1] * GLOB_KV
    col0 = 0
    nq = H
    o_kb, o_vb, o_qc = nq, nq + WIN_KV, nq + 2 * WIN_KV
    o_kc, o_vc = o_qc + H, o_qc + H + GLOB_KV
    scale = HD ** -0.5
    e_qkv = len(kinds) * HD
    e_conv = e_qkv + 2 * C
    assert e_qkv % C == 0 and e_conv % lw["kv_lora"] == 0 and (e_conv + lw["kv_lora"]) % lw["q_lora"] == 0
    y_a = conv_module(p, e_qkv // C, lw["conv_w"], lw["conv_b"], lw["conv_ln_g"], lw["conv_ln_b"], T, C)
    c_ckv = e_conv // lw["kv_lora"]
    c_cq = (e_conv + lw["kv_lora"]) // lw["q_lora"]
    c_kr = (e_conv + lw["kv_lora"] + lw["q_lora"]) // LANES
    rtm = min(512, n)
    qraw = matmul(p, lw["w_uq"], x_cols=[(c_cq, lw["q_lora"])], tm=rtm, tn=lw["w_uq"].shape[1], norm_gain=lw["qa_norm"],
                  name="mla_uq")
    kvup = matmul(p, lw["w_ukv"], x_cols=[(c_ckv, lw["kv_lora"])], tm=rtm, tn=lw["w_ukv"].shape[1],
                  norm_gain=lw["kv_norm"], name="mla_ukv")
    mla_scale = MLA_QK ** -0.5
    if ctx is None:
        qkv, st = qkv_prep(p, col0, kinds, lw["gains"], T, state_heads=(o_kb, o_kb + 1, o_kc, o_kc + 1))
        o_b = attention(qkv, 0, qkv, o_kb, qkv, o_vb, B=B, T=T, KVH=WIN_KV, G=G, dk=HD, dv=HD, tq=T, scale=scale,
                        sink=lw["win_sink"], name="ctx_win_attn")
        o_c = attention(qkv, o_qc // G, qkv, o_kc, qkv, o_vc, B=B, T=T, KVH=GLOB_KV, G=G, dk=HD, dv=HD, tq=T,
                        scale=scale, name="ctx_glob_attn")
        q_d = mla_q_prep(qraw, lw["qn"], T)
        k_d, v_d = mla_k_prep(kvup, p, c_kr, lw["kn"], T)
        o_d = attention(q_d, 0, k_d, 0, v_d, 0, B=B, T=T, KVH=H, G=1, dk=MLA_PAD, dv=MLA_V, tq=T, scale=mla_scale,
                        name="ctx_mla_attn")
        ckv_n = rmsnorm_cols(p, c_ckv, lw["kv_lora"], lw["kv_norm"])
        vb0, vc0 = col0 + o_vb * HD, col0 + o_vc * HD
        kr0 = c_kr * LANES
        state = (st[:, :kvw].reshape(B, T, WIN_KV, HD), p[:, vb0:vb0 + kvw].reshape(B, T, WIN_KV, HD),
                 st[:, kvw:].reshape(B, T, GLOB_KV, HD), p[:, vc0:vc0 + kvw].reshape(B, T, GLOB_KV, HD),
                 ckv_n.reshape(B, T, -1), p[:, kr0:kr0 + MLA_ROPE].reshape(B, T, MLA_ROPE))
    else:
        kb_x, vb_x, kc_x, vc_x, ckv_x, kr_x = ctx
        Tc = kb_x.shape[1]
        flat = lambda a: a.reshape(B * Tc, -1)
        tab_h, tab_r = tabs
        qkv = qkv_prep(p, col0, kinds, lw["gains"], T, rope_tabs=tab_h)
        o_b = window_attention(qkv, 0, o_kb, o_vb, flat(kb_x).astype(BF16), flat(vb_x).astype(BF16), Tc,
                               lw["win_sink"], B=B, T=T, KVH=WIN_KV, G=G, scale=scale)
        o_c = attention(qkv, o_qc // G, qkv, o_kc, qkv, o_vc, B=B, T=T, KVH=GLOB_KV, G=G, dk=HD, dv=HD,
                        tq=512 // G, scale=scale, ctx=(flat(kc_x).astype(BF16), flat(vc_x).astype(BF16), Tc),
                        name="lat_glob_attn")
        q_d = mla_q_prep(qraw, lw["qn"], T, rope_tabs=tab_r)
        k_d, v_d = mla_k_prep(kvup, p, c_kr, lw["kn"], T, rope_tabs=tab_r)
        kvup_x = matmul(flat(ckv_x).astype(BF16), lw["w_ukv"], tm=min(512, B * Tc), tn=lw["w_ukv"].shape[1],
                        name="mla_ukv_cache")
        kr_pad = jnp.pad(flat(kr_x), ((0, 0), (0, LANES - MLA_ROPE)))
        kx_d, vx_d = mla_k_prep(kvup_x, kr_pad, 0, lw["kn"], Tc)
        o_d = attention(q_d, 0, k_d, 0, v_d, 0, B=B, T=T, KVH=H, G=1, dk=MLA_PAD, dv=MLA_V, tq=512, scale=mla_scale,
                        ctx=(kx_d, vx_d, Tc), name="lat_mla_attn")
        state = None
    return (y_a, o_b, o_c, o_d), state


def _peer(h, lw):
    q = matmul(h, lw["wq"], tm=min(1024, h.shape[0]), tn=1024, name="peer_wq")
    i1, i2, g = peer_topk(q, lw["keys"])
    grid = peer_gate_grid(i1, i2, g)
    return peer_dense(h, lw["ut"], lw["v"], grid)


def _trunk_layer(x, mods, lw, B, T, ctx, tabs):
    sh1, sc1, g1, sh2, sc2, g2 = (mods[:, k] for k in range(6))
    n = x.shape[0]
    h = norm_modulate(x, lw["norm1"], sc1, sh1, T)
    mixed, state = _mix_block(h, lw, B, T, ctx, tabs)
    tm_out = min(1024, n if g1.shape[0] == 1 else T)
    x = matmul(list(mixed), lw["w_out"], tm=tm_out, tn=1024, resid=x, gate=g1, T=T, name="w_out")
    h = norm_modulate(x, lw["norm2"], sc2, sh2, T)
    x = gated_add(x, _peer(h, lw), g2, T)
    return x, state


def kernel(x_prompt, x_sample, cache_win_k, cache_win_v, cache_glob_k, cache_glob_v, cache_mla_ckv, cache_mla_krope, c, c_ctx, mod_w, mod_b, norm1_g, norm2_g, w_in, conv_w, conv_b, conv_ln_g, conv_ln_b, win_qn_g, win_kn_g, win_sink, glob_qn_g, glob_kn_g, mla_qa_norm_g, mla_w_uq, mla_kv_norm_g, mla_w_ukv, mla_qn_g, mla_kn_g, w_out, peer_wq, peer_keys, peer_u, peer_v):
    Bp, Tp, D = x_prompt.shape
    Bs, Ts, _ = x_sample.shape
    depth = w_in.shape[0]
    tabs = (rope_tables_head(Ts), rope_tables_tail(Ts))
    rows = 1 + Bs
    rpad = -rows % SUBLANES
    cond = jnp.concatenate([c_ctx[None, :], c, jnp.zeros((rpad, D), F32)], axis=0)
    mods = modulation(cond, mod_w, mod_b).reshape(depth, rows + rpad, 6, D)
    yp = x_prompt.reshape(Bp * Tp, D)
    ys = x_sample.reshape(Bs * Ts, D)
    states = []
    for l in range(depth):
        lw = _layer_weights(l, D, w_in, conv_w, conv_b, conv_ln_g, conv_ln_b, win_qn_g, win_kn_g, win_sink,
                            glob_qn_g, glob_kn_g, mla_qa_norm_g, mla_w_uq, mla_kv_norm_g, mla_w_ukv, mla_qn_g,
                            mla_kn_g, w_out, peer_wq, peer_keys, peer_u, peer_v, norm1_g, norm2_g)
        ctx_l = (cache_win_k[:, l], cache_win_v[:, l], cache_glob_k[:, l], cache_glob_v[:, l],
                 cache_mla_ckv[:, l], cache_mla_krope[:, l])
        yp, st = _trunk_layer(yp, mods[l, 0:1], lw, Bp, Tp, None, None)
        ys, _ = _trunk_layer(ys, mods[l, 1:rows], lw, Bs, Ts, ctx_l, tabs)
        states.append(st)
    new = tuple(jnp.stack([s[k] for s in states], axis=1) for k in range(6))
    return (yp.reshape(Bp, Tp, D), ys.reshape(Bs, Ts, D)) + new
```
